```python
import math
import jax, jax.numpy as jnp
from jax import lax
import numpy as np


D_MODEL = 2048
BATCH = 4
SEQ = 2048
DEPTH = 2
DEC_BATCH = 8
DEC_SEQ = 4
PAST_LEN = 16384
PAGE_SIZE = 128

MIX_WIDTH = D_MODEL
D_SSM = MIX_WIDTH // 2
SSM_HEAD_DIM = 64
SSM_HEADS = D_SSM // SSM_HEAD_DIM
SSM_GROUPS = 2
SSM_STATE = 128
CONV_W = 4
CONV_DIM = D_SSM + 2 * SSM_GROUPS * SSM_STATE
SSD_CHUNK = 128
D_ATTN = MIX_WIDTH - D_SSM
DA_HEAD_DIM = 64
DA_V_DIM = 2 * DA_HEAD_DIM
DA_HEADS = D_ATTN // DA_V_DIM
Q_BLOCK = 128
REL_BUCKETS = 32
REL_MAX_DIST = 128
PROJ_DIM = D_SSM + CONV_DIM + SSM_HEADS + 3 * D_ATTN
PEER_HEADS = 8
PEER_DK = 256
N_KEYS = 128
N_EXPERTS = N_KEYS * N_KEYS
PEER_TOPK = 16
PEER_BLOCK = 128
EPS = 1e-6

kernel_name = "hymba_ssd_diffattn_peer_step"


def rms_normalize(x):
    xf = x.astype(jnp.float32)
    return (xf * lax.rsqrt(jnp.mean(xf * xf, axis=-1, keepdims=True) + EPS)).astype(x.dtype)


def rmsnorm(x, g):
    return rms_normalize(x) * g.astype(x.dtype)


def rel_bucket(rel):
    n = jnp.maximum(rel, 0)
    max_exact = REL_BUCKETS // 2
    nf = jnp.maximum(n, 1).astype(jnp.float32)
    large = max_exact + (jnp.log(nf / max_exact) / math.log(REL_MAX_DIST / max_exact)
                         * (REL_BUCKETS - max_exact)).astype(jnp.int32)
    large = jnp.minimum(large, REL_BUCKETS - 1)
    return jnp.where(n < max_exact, n, large)


def diff_attend(q, k, v, q_pos, k_pos, lam, rel_bias):
    logits = jnp.einsum('bqhmd,bkhmd->bhmqk', q, k).astype(jnp.float32) * (DA_HEAD_DIM ** -0.5)
    bias = jnp.transpose(rel_bias[rel_bucket(q_pos[:, None] - k_pos[None, :])], (2, 0, 1)).astype(jnp.float32)
    causal = k_pos[None, :] <= q_pos[:, None]
    logits = jnp.where(causal, logits + bias[None, :, None], -jnp.inf)
    p = jax.nn.softmax(logits, axis=-1)
    a = p[:, :, 0] - lam * p[:, :, 1]
    return jnp.einsum('bhqk,bkhe->bqhe', a, v.astype(jnp.float32)).astype(v.dtype)


def ssd_scan(x, dt, a, bm, cm, h0, chunk):
    b, L, h, p = x.shape
    nc = L // chunk
    hpg = h // SSM_GROUPS
    xf = x.astype(jnp.float32).reshape(b, nc, chunk, h, p)
    bh = jnp.repeat(bm.astype(jnp.float32), hpg, axis=2).reshape(b, nc, chunk, h, SSM_STATE)
    ch = jnp.repeat(cm.astype(jnp.float32), hpg, axis=2).reshape(b, nc, chunk, h, SSM_STATE)
    dtc = dt.reshape(b, nc, chunk, h)
    a_cum = jnp.cumsum(dtc * a, axis=2)
    seg = a_cum[:, :, :, None, :] - a_cum[:, :, None, :, :]
    tril = (jnp.arange(chunk)[:, None] >= jnp.arange(chunk)[None, :])[:, :, None]
    lmat = jnp.exp(jnp.where(tril, seg, -jnp.inf))
    xdt = xf * dtc[..., None]
    scores = jnp.einsum('bclhn,bcshn->bclsh', ch, bh) * lmat
    y_diag = jnp.einsum('bclsh,bcshp->bclhp', scores, xdt)
    decay_to_end = jnp.exp(a_cum[:, :, -1:, :] - a_cum)
    chunk_states = jnp.einsum('bclhn,bclhp->bchpn', bh * decay_to_end[..., None], xdt)
    chunk_decay = jnp.exp(a_cum[:, :, -1, :])

    def step(hc, inp):
        dec, st = inp
        return dec[:, :, None, None] * hc + st, hc

    h_fin, h_prev = lax.scan(step, h0.astype(jnp.float32),
                             (jnp.moveaxis(chunk_decay, 1, 0), jnp.moveaxis(chunk_states, 1, 0)))
    h_prev = jnp.moveaxis(h_prev, 0, 1)
    y_off = jnp.einsum('bclhn,bchpn->bclhp', ch * jnp.exp(a_cum)[..., None], h_prev)
    y = (y_diag + y_off).reshape(b, L, h, p)
    return y.astype(x.dtype), h_fin.astype(x.dtype)


def mixer(n, p, lam_init, rel_bias, conv_buf, ssm_h, past_k, past_v, q_offset, chunk):
    bsz, L, _ = n.shape
    proj = n @ p['w_in']
    o1 = D_SSM
    o2 = o1 + CONV_DIM
    o3 = o2 + SSM_HEADS
    o4 = o3 + D_ATTN
    o5 = o4 + D_ATTN
    z, xbc, dt_raw = proj[..., :o1], proj[..., o1:o2], proj[..., o2:o3]
    q, k, v = proj[..., o3:o4], proj[..., o4:o5], proj[..., o5:]

    xpad = jnp.concatenate([conv_buf.astype(xbc.dtype), xbc], axis=1)
    new_buf = xpad[:, -(CONV_W - 1):]
    xc = lax.conv_general_dilated(xpad, p['conv_w'][:, None, :], (1,), 'VALID',
                                  dimension_numbers=('NWC', 'WIO', 'NWC'),
                                  feature_group_count=CONV_DIM)
    xc = jax.nn.silu(xc + p['conv_b'])
    gn = SSM_GROUPS * SSM_STATE
    xs = xc[..., :D_SSM].reshape(bsz, L, SSM_HEADS, SSM_HEAD_DIM)
    bm = xc[..., D_SSM:D_SSM + gn].reshape(bsz, L, SSM_GROUPS, SSM_STATE)
    cm = xc[..., D_SSM + gn:].reshape(bsz, L, SSM_GROUPS, SSM_STATE)
    dt = jax.nn.softplus(dt_raw.astype(jnp.float32) + p['dt_bias'].astype(jnp.float32))
    a = -jnp.exp(p['a_log'].astype(jnp.float32))
    y, h_fin = ssd_scan(xs, dt, a, bm, cm, ssm_h, chunk)
    y = (y + p['d_skip'][:, None] * xs).reshape(bsz, L, D_SSM) * jax.nn.silu(z)
    y = rms_normalize(y.reshape(bsz, L, SSM_GROUPS, D_SSM // SSM_GROUPS)).reshape(bsz, L, D_SSM) * p['ssm_norm_g']

    lq = p['lam_qk'].astype(jnp.float32)
    lam = jnp.exp(jnp.sum(lq[0] * lq[1])) - jnp.exp(jnp.sum(lq[2] * lq[3])) + lam_init
    q = q.reshape(bsz, L, DA_HEADS, 2, DA_HEAD_DIM)
    k = k.reshape(bsz, L, DA_HEADS, 2, DA_HEAD_DIM)
    v = v.reshape(bsz, L, DA_HEADS, DA_V_DIM)
    if past_k is None:
        k_pos = jnp.arange(L)
        nb = L // Q_BLOCK
        qb = jnp.moveaxis(q.reshape(bsz, nb, Q_BLOCK, DA_HEADS, 2, DA_HEAD_DIM), 1, 0)

        def blk(args):
            qi, i = args
            return diff_attend(qi, k, v, i * Q_BLOCK + jnp.arange(Q_BLOCK), k_pos, lam, rel_bias)

        o = lax.map(blk, (qb, jnp.arange(nb)))
        o = jnp.moveaxis(o, 0, 1).reshape(bsz, L, DA_HEADS, DA_V_DIM)
    else:
        keys = jnp.concatenate([past_k.reshape(bsz, -1, DA_HEADS, 2, DA_HEAD_DIM).astype(k.dtype), k], axis=1)
        vals = jnp.concatenate([past_v.reshape(bsz, -1, DA_HEADS, DA_V_DIM).astype(v.dtype), v], axis=1)
        k_pos = jnp.arange(keys.shape[1])
        o = diff_attend(q, keys, vals, q_offset + jnp.arange(L), k_pos, lam, rel_bias)
    o = rmsnorm(o, p['subln_g']) * (1.0 - lam_init)

    out = jnp.concatenate([y, o.reshape(bsz, L, D_ATTN)], axis=-1) @ p['w_out']
    return out, k.reshape(bsz, L, DA_HEADS, 2 * DA_HEAD_DIM), v, h_fin, new_buf


def peer_tokens(xt, wq, sk1, sk2, u_tab, v_tab):
    t = xt.shape[0]
    q = (xt @ wq).astype(jnp.float32).reshape(t, PEER_HEADS, PEER_DK)
    half = PEER_DK // 2
    s1 = jnp.einsum('thd,hkd->thk', q[..., :half], sk1.astype(jnp.float32))
    s2 = jnp.einsum('thd,hkd->thk', q[..., half:], sk2.astype(jnp.float32))
    v1, i1 = lax.top_k(s1, PEER_TOPK)
    v2, i2 = lax.top_k(s2, PEER_TOPK)
    cand = (v1[..., :, None] + v2[..., None, :]).reshape(t, PEER_HEADS, PEER_TOPK * PEER_TOPK)
    sv, si = lax.top_k(cand, PEER_TOPK)
    e = (jnp.take_along_axis(i1, si // PEER_TOPK, axis=-1) * N_KEYS
         + jnp.take_along_axis(i2, si % PEER_TOPK, axis=-1))
    g = jax.nn.softmax(sv, axis=-1)
    act = jax.nn.gelu(jnp.einsum('thkd,td->thk', u_tab[e], xt).astype(jnp.float32), approximate=False)
    out = jnp.einsum('thk,thkd->td', g * act, v_tab[e].astype(jnp.float32))
    return out.astype(xt.dtype)


def peer(x, wq, sk1, sk2, u_tab, v_tab):
    bsz, L, d = x.shape
    xt = x.reshape(-1, d)
    t = xt.shape[0]
    if t % PEER_BLOCK == 0 and t > PEER_BLOCK:
        out = lax.map(lambda blk: peer_tokens(blk, wq, sk1, sk2, u_tab, v_tab),
                      xt.reshape(t // PEER_BLOCK, PEER_BLOCK, d)).reshape(t, d)
    else:
        out = peer_tokens(xt, wq, sk1, sk2, u_tab, v_tab)
    return out.reshape(bsz, L, d)


def setup_inputs(seed: int = 0) -> dict:
    key = jax.random.key(seed)
    ks = jax.random.split(key, 32)
    nrm = jax.random.normal
    n_pages = PAST_LEN // PAGE_SIZE
    n_used = DEC_BATCH * n_pages
    n_pool = n_used + max(1, n_used // 4)
    page_table = jax.random.permutation(ks[0], n_pool)[:n_used].reshape(DEC_BATCH, n_pages).astype(jnp.int32)
    dt0 = jnp.exp(jax.random.uniform(ks[1], (DEPTH, SSM_HEADS), minval=math.log(1e-3), maxval=math.log(1e-1)))
    return {
        "x_prompt": nrm(ks[2], (BATCH, SEQ, D_MODEL), jnp.float32),
        "x_sample": nrm(ks[3], (DEC_BATCH, DEC_SEQ, D_MODEL), jnp.float32),
        "cache_k": nrm(ks[4], (DEPTH, n_pool, PAGE_SIZE, DA_HEADS, 2 * DA_HEAD_DIM), jnp.float32),
        "cache_v": nrm(ks[5], (DEPTH, n_pool, PAGE_SIZE, DA_HEADS, DA_V_DIM), jnp.float32),
        "state_ssm": 0.5 * nrm(ks[6], (DEPTH, DEC_BATCH, SSM_HEADS, SSM_HEAD_DIM, SSM_STATE), jnp.float32),
        "state_conv": nrm(ks[7], (DEPTH, DEC_BATCH, CONV_W - 1, CONV_DIM), jnp.float32),
        "page_table": page_table,
        "rel_bias": 0.5 * nrm(ks[8], (REL_BUCKETS, DA_HEADS), jnp.float32),
        "norm_attn_g": 1.0 + 0.02 * nrm(ks[9], (DEPTH, D_MODEL), jnp.float32),
        "w_in": nrm(ks[10], (DEPTH, D_MODEL, PROJ_DIM), jnp.float32) * D_MODEL ** -0.5,
        "conv_w": nrm(ks[11], (DEPTH, CONV_W, CONV_DIM), jnp.float32) * CONV_W ** -0.5,
        "conv_b": 0.01 * nrm(ks[12], (DEPTH, CONV_DIM), jnp.float32),
        "a_log": jnp.log(jax.random.uniform(ks[13], (DEPTH, SSM_HEADS), minval=1.0, maxval=16.0)),
        "dt_bias": dt0 + jnp.log(-jnp.expm1(-dt0)),
        "d_skip": 1.0 + 0.1 * nrm(ks[14], (DEPTH, SSM_HEADS), jnp.float32),
        "ssm_norm_g": 1.0 + 0.02 * nrm(ks[15], (DEPTH, D_SSM), jnp.float32),
        "lam_qk": 0.1 * nrm(ks[16], (DEPTH, 4, DA_HEAD_DIM), jnp.float32),
        "subln_g": 1.0 + 0.02 * nrm(ks[17], (DEPTH, DA_V_DIM), jnp.float32),
        "w_out": nrm(ks[18], (DEPTH, MIX_WIDTH, D_MODEL), jnp.float32) * MIX_WIDTH ** -0.5,
        "norm_ffn_g": 1.0 + 0.02 * nrm(ks[19], (DEPTH, D_MODEL), jnp.float32),
        "peer_wq": nrm(ks[20], (DEPTH, D_MODEL, PEER_HEADS * PEER_DK), jnp.float32) * D_MODEL ** -0.5,
        "peer_sk1": nrm(ks[21], (DEPTH, PEER_HEADS, N_KEYS, PEER_DK // 2), jnp.float32) * (PEER_DK // 2) ** -0.5,
        "peer_sk2": nrm(ks[22], (DEPTH, PEER_HEADS, N_KEYS, PEER_DK // 2), jnp.float32) * (PEER_DK // 2) ** -0.5,
        "peer_u": nrm(ks[23], (DEPTH, N_EXPERTS, D_MODEL), jnp.float32) * D_MODEL ** -0.5,
        "peer_v": nrm(ks[24], (DEPTH, N_EXPERTS, D_MODEL), jnp.float32) * (PEER_HEADS * PEER_TOPK) ** -0.5,
        "norm_final_g": 1.0 + 0.02 * nrm(ks[25], (D_MODEL,), jnp.float32),
    }


def reference(x_prompt, x_sample, cache_k, cache_v, state_ssm, state_conv, page_table, rel_bias,
              norm_attn_g, w_in, conv_w, conv_b, a_log, dt_bias, d_skip, ssm_norm_g, lam_qk, subln_g,
              w_out, norm_ffn_g, peer_wq, peer_sk1, peer_sk2, peer_u, peer_v, norm_final_g):
    xp, xs = x_prompt, x_sample
    bp = x_prompt.shape[0]
    conv0 = jnp.zeros((bp, CONV_W - 1, CONV_DIM), x_prompt.dtype)
    ssm0 = jnp.zeros((bp, SSM_HEADS, SSM_HEAD_DIM, SSM_STATE), jnp.float32)
    kp_l, vp_l, hp_l, cp_l = [], [], [], []
    ks_l, vs_l, hs_l, cs_l = [], [], [], []
    for l in range(DEPTH):
        p = {"w_in": w_in[l], "conv_w": conv_w[l], "conv_b": conv_b[l], "a_log": a_log[l],
             "dt_bias": dt_bias[l], "d_skip": d_skip[l], "ssm_norm_g": ssm_norm_g[l],
             "lam_qk": lam_qk[l], "subln_g": subln_g[l], "w_out": w_out[l]}
        lam_init = 0.8 - 0.6 * math.exp(-0.3 * l)
        o, k_new, v_new, h_new, buf_new = mixer(rmsnorm(xp, norm_attn_g[l]), p, lam_init, rel_bias,
                                                conv0, ssm0, None, None, 0, SSD_CHUNK)
        xp = xp + o
        xp = xp + peer(rmsnorm(xp, norm_ffn_g[l]), peer_wq[l], peer_sk1[l], peer_sk2[l], peer_u[l], peer_v[l])
        kp_l.append(k_new); vp_l.append(v_new); hp_l.append(h_new); cp_l.append(buf_new)
        past_k = cache_k[l][page_table]
        past_v = cache_v[l][page_table]
        past_len = page_table.shape[1] * PAGE_SIZE
        o, k_new, v_new, h_new, buf_new = mixer(rmsnorm(xs, norm_attn_g[l]), p, lam_init, rel_bias,
                                                state_conv[l], state_ssm[l], past_k, past_v,
                                                past_len, xs.shape[1])
        xs = xs + o
        xs = xs + peer(rmsnorm(xs, norm_ffn_g[l]), peer_wq[l], peer_sk1[l], peer_sk2[l], peer_u[l], peer_v[l])
        ks_l.append(k_new); vs_l.append(v_new); hs_l.append(h_new); cs_l.append(buf_new)
    y_prompt = rmsnorm(xp, norm_final_g)
    y_sample = rmsnorm(xs, norm_final_g)
    return (y_prompt, y_sample,
            jnp.stack(kp_l), jnp.stack(vp_l), jnp.stack(hp_l), jnp.stack(cp_l),
            jnp.stack(ks_l), jnp.stack(vs_l), jnp.stack(hs_l), jnp.stack(cs_l))
```

```python
import functools
import math

import numpy as np
import jax
import jax.numpy as jnp
from jax import lax
from jax.experimental import pallas as pl
from jax.experimental.pallas import tpu as pltpu

D_MODEL = 2048
DEPTH = 2
PAGE_SIZE = 128
D_SSM = 1024
SSM_HEAD_DIM = 64
SSM_HEADS = 16
SSM_GROUPS = 2
SSM_STATE = 128
CONV_W = 4
CONV_DIM = D_SSM + 2 * SSM_GROUPS * SSM_STATE
SSD_CHUNK = 128
D_ATTN = 1024
DA_HEAD_DIM = 64
DA_V_DIM = 128
DA_HEADS = 8
REL_BUCKETS = 32
REL_MAX_DIST = 128
PEER_HEADS = 8
PEER_DK = 256
N_KEYS = 128
N_EXPERTS = N_KEYS * N_KEYS
PEER_TOPK = 16
EPS = 1e-6

LANES = 128
SUBLANES = 8
VMEM_LIMIT = 56 * 1024 * 1024
NEG = -1e30

F32 = jnp.float32
BF16 = jnp.bfloat16


def _cparams(sem):
    return pltpu.CompilerParams(dimension_semantics=sem, vmem_limit_bytes=VMEM_LIMIT)


def _dot(a, b):
    return jnp.dot(a, b, preferred_element_type=F32)


def _dot_nt(a, b):
    return lax.dot_general(a, b, (((1,), (1,)), ((), ())), preferred_element_type=F32)


def _rel_thresholds():
    n = np.arange(0, REL_MAX_DIST + 1)
    max_exact = REL_BUCKETS // 2
    nf = np.maximum(n, 1).astype(np.float32)
    large = max_exact + (np.log(nf / np.float32(max_exact)) / np.float32(math.log(REL_MAX_DIST / max_exact))
                         * np.float32(REL_BUCKETS - max_exact)).astype(np.int32)
    large = np.minimum(large, REL_BUCKETS - 1)
    bucket = np.where(n < max_exact, n, large)
    thr = []
    for b in range(1, REL_BUCKETS):
        thr.append(int(n[bucket >= b][0]))
    return thr


_REL_THR = _rel_thresholds()


def _bias_from_dist(dist, rb_ref, h):
    out = jnp.full(dist.shape, rb_ref[0, h], F32)
    for b in range(1, REL_BUCKETS):
        out = jnp.where(dist >= _REL_THR[b - 1], rb_ref[b, h], out)
    return out


def _norm_kernel(x_ref, g_ref, o_ref):
    x = x_ref[...]
    o_ref[...] = ((x * lax.rsqrt(jnp.mean(x * x, axis=-1, keepdims=True) + EPS)) * g_ref[...]).astype(o_ref.dtype)


def _norm(x, g, out_dtype):
    t, d = x.shape
    tm = min(t, 512)
    return pl.pallas_call(
        _norm_kernel,
        grid=(t // tm,),
        in_specs=[pl.BlockSpec((tm, d), lambda i: (i, 0)), pl.BlockSpec((1, d), lambda i: (0, 0))],
        out_specs=pl.BlockSpec((tm, d), lambda i: (i, 0)),
        out_shape=jax.ShapeDtypeStruct((t, d), out_dtype),
        compiler_params=_cparams(("parallel",)),
        name="rmsnorm",
    )(x, g.reshape(1, d))


def _mm_kernel(a_ref, w_ref, *o_refs, splits):
    a = a_ref[...]
    for (s, e), o_ref in zip(splits, o_refs):
        o_ref[...] = _dot(a, w_ref[:, s:e]).astype(o_ref.dtype)


def _mm(a, w, widths, dtypes, name):
    t, k = a.shape
    tm = min(t, 512)
    splits, s = [], 0
    for wd in widths:
        splits.append((s, s + wd))
        s += wd
    return pl.pallas_call(
        functools.partial(_mm_kernel, splits=tuple(splits)),
        grid=(t // tm,),
        in_specs=[pl.BlockSpec((tm, k), lambda i: (i, 0)), pl.BlockSpec(w.shape, lambda i: (0, 0))],
        out_specs=[pl.BlockSpec((tm, wd), lambda i: (i, 0)) for wd in widths],
        out_shape=[jax.ShapeDtypeStruct((t, wd), dt) for wd, dt in zip(widths, dtypes)],
        compiler_params=_cparams(("parallel",)),
        name=name,
    )(a, w)


def _outproj_kernel(x_ref, y_ref, o_ref, w1_ref, w2_ref, g_ref, xo_ref, xn_ref):
    x = x_ref[...] + _dot(y_ref[...], w1_ref[...]) + _dot(o_ref[...], w2_ref[...])
    xo_ref[...] = x
    xn_ref[...] = ((x * lax.rsqrt(jnp.mean(x * x, axis=-1, keepdims=True) + EPS)) * g_ref[...]).astype(xn_ref.dtype)


def _outproj(x, y, o, w1, w2, g):
    t, d = x.shape
    tm = min(t, 512)
    row = lambda i: (i, 0)
    fix = lambda i: (0, 0)
    return pl.pallas_call(
        _outproj_kernel,
        grid=(t // tm,),
        in_specs=[pl.BlockSpec((tm, d), row), pl.BlockSpec((tm, D_SSM), row), pl.BlockSpec((tm, D_ATTN), row),
                  pl.BlockSpec(w1.shape, fix), pl.BlockSpec(w2.shape, fix), pl.BlockSpec((1, d), fix)],
        out_specs=[pl.BlockSpec((tm, d), row), pl.BlockSpec((tm, d), row)],
        out_shape=[jax.ShapeDtypeStruct((t, d), F32), jax.ShapeDtypeStruct((t, d), BF16)],
        compiler_params=_cparams(("parallel",)),
        name="outproj",
    )(x, y, o, w1, w2, g.reshape(1, d))


def _ssd_kernel(xbc_ref, z_ref, dt_ref, cw_ref, cb_ref, dtb_ref, a_ref, dsk_ref, ng_ref, h0_ref, cbuf_ref,
                y_ref, hfin_ref, xpad_scr, h_scr, *, valid_len):
    c = pl.program_id(1)
    lc = SSD_CHUNK
    pairs = SSM_HEADS // 2

    @pl.when(c == 0)
    def _():
        xpad_scr[0:SUBLANES, :] = cbuf_ref[0]
        for j in range(pairs):
            hp = jnp.concatenate([h0_ref[0, 2 * j], h0_ref[0, 2 * j + 1]], axis=0)
            h_scr[j] = hp.T

    xpad_scr[SUBLANES:SUBLANES + lc, :] = xbc_ref[0]
    acc = jnp.broadcast_to(cb_ref[...], (lc, CONV_DIM))
    for w in range(CONV_W):
        lo = SUBLANES - (CONV_W - 1) + w
        acc = acc + xpad_scr[lo:lo + lc, :] * cw_ref[w:w + 1, :]
    xpad_scr[0:SUBLANES, :] = xpad_scr[lc:lc + SUBLANES, :]
    xc = acc * jax.nn.sigmoid(acc)

    gn = SSM_GROUPS * SSM_STATE
    xs = xc[:, :D_SSM]
    bm = xc[:, D_SSM:D_SSM + gn]
    cm = xc[:, D_SSM + gn:]

    rows = lax.broadcasted_iota(jnp.int32, (lc, LANES), 0)
    cols = lax.broadcasted_iota(jnp.int32, (lc, LANES), 1)
    tril = rows >= cols
    lane_lo = cols < SSM_HEAD_DIM

    xdt_in = dt_ref[0] + dtb_ref[...]
    dt = jnp.maximum(xdt_in, 0.0) + jnp.log1p(jnp.exp(-jnp.abs(xdt_in)))
    dt = jnp.where(rows < valid_len, dt, 0.0)
    da = dt * a_ref[...]
    cum = jnp.dot(tril.astype(F32), da, preferred_element_type=F32, precision=lax.Precision.HIGHEST)
    cum_t = cum.T
    last = cum[lc - 1:lc, :]

    cb, bt = [], []
    for g in range(SSM_GROUPS):
        b_g = bm[:, g * SSM_STATE:(g + 1) * SSM_STATE]
        c_g = cm[:, g * SSM_STATE:(g + 1) * SSM_STATE]
        cb.append(_dot_nt(c_g.astype(BF16), b_g.astype(BF16)))
        bt.append(b_g.T.astype(BF16))

    z = z_ref[0]
    gate = z * jax.nn.sigmoid(z)
    ys = []
    for j in range(pairs):
        ha, hb = 2 * j, 2 * j + 1
        g = ha // (SSM_HEADS // SSM_GROUPS)
        c_g = cm[:, g * SSM_STATE:(g + 1) * SSM_STATE]
        cum_a, cum_b = cum[:, ha:ha + 1], cum[:, hb:hb + 1]
        lm_a = jnp.exp(jnp.where(tril, cum_a - cum_t[ha:ha + 1, :], NEG))
        lm_b = jnp.exp(jnp.where(tril, cum_b - cum_t[hb:hb + 1, :], NEG))
        sc = jnp.concatenate([cb[g] * lm_a, cb[g] * lm_b], axis=1).astype(BF16)
        xs_p = xs[:, j * LANES:(j + 1) * LANES]
        xdt = xs_p * jnp.where(lane_lo, dt[:, ha:ha + 1], dt[:, hb:hb + 1])
        rhs = jnp.concatenate([jnp.where(lane_lo, xdt, 0.0), jnp.where(lane_lo, 0.0, xdt)], axis=0).astype(BF16)
        y_diag = _dot(sc, rhs)
        last_a, last_b = last[:, ha:ha + 1], last[:, hb:hb + 1]
        dte = jnp.exp(jnp.where(lane_lo, last_a - cum_a, last_b - cum_b))
        st = _dot(bt[g], (xdt * dte).astype(BF16))
        cexp = jnp.concatenate([c_g * jnp.exp(cum_a), c_g * jnp.exp(cum_b)], axis=1).astype(BF16)
        h_t = h_scr[j]
        rhs2 = jnp.concatenate([jnp.where(lane_lo, h_t, 0.0), jnp.where(lane_lo, 0.0, h_t)], axis=0).astype(BF16)
        y_off = _dot(cexp, rhs2)
        decay = jnp.where(lane_lo[0:1, :], jnp.exp(last_a), jnp.exp(last_b))
        h_scr[j] = decay * h_t + st
        dsk = jnp.where(lane_lo[0:1, :], dsk_ref[0:1, ha:ha + 1], dsk_ref[0:1, hb:hb + 1])
        ys.append(y_diag + y_off + dsk * xs_p)

    half = D_SSM // SSM_GROUPS
    outs = []
    for g in range(SSM_GROUPS):
        yg = jnp.concatenate(ys[g * (pairs // SSM_GROUPS):(g + 1) * (pairs // SSM_GROUPS)], axis=1)
        yg = yg * gate[:, g * half:(g + 1) * half]
        yg = yg * lax.rsqrt(jnp.mean(yg * yg, axis=-1, keepdims=True) + EPS)
        outs.append(yg * ng_ref[:, g * half:(g + 1) * half])
    y_ref[0] = jnp.concatenate(outs, axis=1).astype(y_ref.dtype)

    @pl.when(c == pl.num_programs(1) - 1)
    def _():
        for j in range(pairs):
            t = h_scr[j].T
            hfin_ref[0, 2 * j] = t[:SSM_HEAD_DIM]
            hfin_ref[0, 2 * j + 1] = t[SSM_HEAD_DIM:]


def _ssd(xbc, z, dt, conv_w, conv_b, dt_bias, a_log, d_skip, norm_g, h0, cbuf, valid_len):
    b, seq, _ = xbc.shape
    nc = seq // SSD_CHUNK
    pad1 = lambda v: jnp.pad(v.astype(F32), (0, LANES - SSM_HEADS)).reshape(1, LANES)
    cbuf8 = jnp.pad(cbuf, ((0, 0), (SUBLANES - (CONV_W - 1), 0), (0, 0)))
    blk = lambda w: pl.BlockSpec((1, SSD_CHUNK, w), lambda i, c: (i, c, 0))
    fix = lambda shape: pl.BlockSpec(shape, lambda i, c: tuple(0 for _ in shape))
    per_b = lambda shape: pl.BlockSpec((1,) + shape, lambda i, c: (i,) + tuple(0 for _ in shape))
    return pl.pallas_call(
        functools.partial(_ssd_kernel, valid_len=valid_len),
        grid=(b, nc),
        in_specs=[blk(CONV_DIM), blk(D_SSM), blk(LANES),
                  fix((CONV_W, CONV_DIM)), fix((1, CONV_DIM)), fix((1, LANES)), fix((1, LANES)), fix((1, LANES)),
                  fix((1, D_SSM)), per_b((SSM_HEADS, SSM_HEAD_DIM, SSM_STATE)), per_b((SUBLANES, CONV_DIM))],
        out_specs=[blk(D_SSM), per_b((SSM_HEADS, SSM_HEAD_DIM, SSM_STATE))],
        out_shape=[jax.ShapeDtypeStruct((b, seq, D_SSM), BF16),
                   jax.ShapeDtypeStruct((b, SSM_HEADS, SSM_HEAD_DIM, SSM_STATE), F32)],
        scratch_shapes=[pltpu.VMEM((SUBLANES + SSD_CHUNK, CONV_DIM), F32),
                        pltpu.VMEM((SSM_HEADS // 2, SSM_STATE, 2 * SSM_HEAD_DIM), F32)],
        compiler_params=_cparams(("parallel", "arbitrary")),
        name="conv_ssd",
    )(xbc, z, dt, conv_w, conv_b.reshape(1, CONV_DIM), pad1(dt_bias), pad1(-jnp.exp(a_log.astype(F32))),
      pad1(d_skip), norm_g.reshape(1, D_SSM), h0, cbuf8)


ATT_T = 512


def _softmax_step(s, v, m_scr, l_scr, acc_scr):
    reps = s.shape[1] // LANES
    m_prev = m_scr[...]
    m_new = jnp.maximum(m_prev, jnp.max(s, axis=1, keepdims=True))
    alpha = jnp.exp(m_prev - m_new)
    p = jnp.exp(s - jnp.concatenate([m_new] * reps, axis=1))
    l_scr[...] = alpha * l_scr[...] + jnp.sum(p, axis=1, keepdims=True)
    acc_scr[...] = alpha * acc_scr[...] + _dot(p.astype(BF16), v)
    m_scr[...] = m_new


def _attn_p_kernel(rb_ref, lam_ref, q_ref, k_ref, v_ref, g_ref, o_ref,
                   r_scr, m1, l1, a1, m2, l2, a2, *, out_scale):
    h, b, qi, ki = pl.program_id(0), pl.program_id(1), pl.program_id(2), pl.program_id(3)
    t = ATT_T
    rows = lax.broadcasted_iota(jnp.int32, (t, t), 0)
    cols = lax.broadcasted_iota(jnp.int32, (t, t), 1)

    @pl.when((b == 0) & (qi == 0) & (ki == 0))
    def _():
        r_scr[...] = _bias_from_dist((rows - cols) & (t - 1), rb_ref, h)

    @pl.when(ki == 0)
    def _():
        for m, l, a in ((m1, l1, a1), (m2, l2, a2)):
            m[...] = jnp.full(m.shape, NEG, F32)
            l[...] = jnp.zeros(l.shape, F32)
            a[...] = jnp.zeros(a.shape, F32)

    def update(bias):
        q = q_ref[...]
        k = k_ref[...].astype(BF16)
        v = v_ref[...].astype(BF16)
        lane_lo = lax.broadcasted_iota(jnp.int32, q.shape, 1) < DA_HEAD_DIM
        zero = jnp.zeros_like(q)
        s1 = _dot_nt(jnp.where(lane_lo, q, zero), k) + bias
        _softmax_step(s1, v, m1, l1, a1)
        s2 = _dot_nt(jnp.where(lane_lo, zero, q), k) + bias
        _softmax_step(s2, v, m2, l2, a2)

    far = rb_ref[REL_BUCKETS - 1, h]

    @pl.when(ki == qi)
    def _():
        update(jnp.where(cols <= rows, r_scr[...], NEG))

    @pl.when(ki == qi - 1)
    def _():
        update(jnp.where(cols > rows, r_scr[...], far))

    @pl.when(ki < qi - 1)
    def _():
        update(far)

    @pl.when(ki == qi)
    def _():
        o = a1[...] / l1[...] - lam_ref[0] * (a2[...] / l2[...])
        o = (o * lax.rsqrt(jnp.mean(o * o, axis=-1, keepdims=True) + EPS)) * g_ref[...]
        o_ref[...] = (o * out_scale).astype(o_ref.dtype)


def _attn_prompt(q, k, v, rel_bias, lam, subln_g, out_scale, batch, seq):
    t = ATT_T
    nq = seq // t
    smem = pl.BlockSpec(memory_space=pltpu.SMEM)
    qmap = lambda h, b, qi, ki: (b * nq + qi, h)
    kmap = lambda h, b, qi, ki: (b * nq + jnp.minimum(ki, qi), h)
    stat = pltpu.VMEM((t, LANES), F32)
    return pl.pallas_call(
        functools.partial(_attn_p_kernel, out_scale=out_scale),
        grid=(DA_HEADS, batch, nq, nq),
        in_specs=[smem, smem, pl.BlockSpec((t, LANES), qmap), pl.BlockSpec((t, LANES), kmap),
                  pl.BlockSpec((t, LANES), kmap), pl.BlockSpec((1, LANES), lambda h, b, qi, ki: (0, 0))],
        out_specs=pl.BlockSpec((t, LANES), qmap),
        out_shape=jax.ShapeDtypeStruct((batch * seq, D_ATTN), BF16),
        scratch_shapes=[pltpu.VMEM((t, t), F32), stat, stat, stat, stat, stat, stat],
        compiler_params=_cparams(("arbitrary", "arbitrary", "arbitrary", "arbitrary")),
        name="diff_attn_prompt",
    )(rel_bias, lam.reshape(1), q, k, v, subln_g.reshape(1, LANES))


PAGES_PER_STEP = 8
DEC_ROWS = SUBLANES


def _attn_s_kernel(pt_ref, rb_ref, lam_ref, q_ref, kn_ref, vn_ref, g_ref, *refs, layer, n_pages, dec_seq, out_scale):
    del layer
    pps = PAGES_PER_STEP
    k_refs, v_refs = refs[:pps], refs[pps:2 * pps]
    o_ref, m_scr, l_scr, acc_scr = refs[2 * pps:]
    s = pl.program_id(1)
    nsteps = pl.num_programs(1)
    past_len = n_pages * PAGE_SIZE

    @pl.when(s == 0)
    def _():
        m_scr[...] = jnp.full(m_scr.shape, NEG, F32)
        l_scr[...] = jnp.zeros(l_scr.shape, F32)
        acc_scr[...] = jnp.zeros(acc_scr.shape, F32)

    rows = lax.broadcasted_iota(jnp.int32, (DEC_ROWS, LANES), 0)
    cols = lax.broadcasted_iota(jnp.int32, (DEC_ROWS, LANES), 1)
    qrow = jnp.where(rows < dec_seq, rows, rows - dec_seq)
    first = rows < dec_seq

    def head_q(h):
        qh = q_ref[0, :, h * LANES:(h + 1) * LANES]
        return jnp.where(first == (cols < DA_HEAD_DIM), qh, jnp.zeros_like(qh))

    def step(h, logits, v):
        m_prev = m_scr[h]
        m_new = jnp.maximum(m_prev, jnp.max(logits, axis=1, keepdims=True))
        alpha = jnp.exp(m_prev - m_new)
        p = jnp.exp(logits - m_new)
        l_scr[h] = alpha * l_scr[h] + jnp.sum(p, axis=1, keepdims=True)
        acc_scr[h] = alpha * acc_scr[h] + _dot(p.astype(BF16), v)
        m_scr[h] = m_new

    for r in range(pps):
        for h in range(DA_HEADS):
            kh = k_refs[r][:, h, :].astype(BF16)
            vh = v_refs[r][:, h, :].astype(BF16)
            if r == pps - 1:
                dist = past_len + qrow - ((s * pps + r) * PAGE_SIZE + cols)
                bias = _bias_from_dist(jnp.minimum(dist, REL_MAX_DIST), rb_ref, h)
            else:
                bias = rb_ref[REL_BUCKETS - 1, h]
            step(h, _dot_nt(head_q(h), kh) + bias, vh)

    @pl.when(s == nsteps - 1)
    def _():
        for h in range(DA_HEADS):
            kh = kn_ref[0, :, h * LANES:(h + 1) * LANES].astype(BF16)
            vh = vn_ref[0, :, h * LANES:(h + 1) * LANES].astype(BF16)
            valid = (cols < dec_seq) & (cols <= qrow)
            bias = _bias_from_dist(jnp.maximum(qrow - cols, 0), rb_ref, h)
            logits = jnp.where(valid, _dot_nt(head_q(h), kh) + bias, NEG)
            step(h, logits, vh)
            t = acc_scr[h] / l_scr[h]
            o = t - lam_ref[0] * pltpu.roll(t, dec_seq, axis=0)
            o = (o * lax.rsqrt(jnp.mean(o * o, axis=-1, keepdims=True) + EPS)) * g_ref[...]
            o_ref[0, :, h * LANES:(h + 1) * LANES] = (o * out_scale).astype(o_ref.dtype)


def _attn_sample(q8, kn8, vn8, cache_k, cache_v, page_table, layer, rel_bias, lam, subln_g, out_scale, dec_seq):
    batch, n_pages = page_table.shape
    pps = PAGES_PER_STEP
    smem = pl.BlockSpec(memory_space=pltpu.SMEM)
    row = lambda b, s, pt: (b, 0, 0)
    page_specs = [pl.BlockSpec((None, None, PAGE_SIZE, DA_HEADS, LANES),
                               functools.partial(lambda b, s, pt, r: (layer, pt[b, s * pps + r], 0, 0, 0), r=r))
                  for r in range(pps)]
    stat = pltpu.VMEM((DA_HEADS, DEC_ROWS, LANES), F32)
    grid_spec = pltpu.PrefetchScalarGridSpec(
        num_scalar_prefetch=1,
        grid=(batch, n_pages // pps),
        in_specs=[smem, smem, pl.BlockSpec((1, DEC_ROWS, D_ATTN), row), pl.BlockSpec((1, PAGE_SIZE, D_ATTN), row),
                  pl.BlockSpec((1, PAGE_SIZE, D_ATTN), row), pl.BlockSpec((1, LANES), lambda b, s, pt: (0, 0))]
                 + page_specs + page_specs,
        out_specs=pl.BlockSpec((1, DEC_ROWS, D_ATTN), row),
        scratch_shapes=[stat, stat, stat],
    )
    return pl.pallas_call(
        functools.partial(_attn_s_kernel, layer=layer, n_pages=n_pages, dec_seq=dec_seq, out_scale=out_scale),
        grid_spec=grid_spec,
        out_shape=jax.ShapeDtypeStruct((batch, DEC_ROWS, D_ATTN), BF16),
        compiler_params=_cparams(("parallel", "arbitrary")),
        name="diff_attn_sample",
    )(page_table, rel_bias, lam.reshape(1), q8, kn8, vn8, subln_g.reshape(1, LANES),
      *([cache_k] * pps), *([cache_v] * pps))


PEER_CANDS = [(i, j) for i in range(PEER_TOPK) for j in range(PEER_TOPK) if (i + 1) * (j + 1) <= PEER_TOPK]
CAND_ROWS = -(-len(PEER_CANDS) // SUBLANES) * SUBLANES


def _extract_top(x, count):
    rows = lax.broadcasted_iota(jnp.int32, x.shape, 0)
    out = []
    for r in range(count):
        m = jnp.max(x, axis=0, keepdims=True)
        out.append(m)
        if r + 1 < count:
            first = jnp.min(jnp.where(x == m, rows, x.shape[0]), axis=0, keepdims=True)
            x = jnp.where(rows == first, -jnp.inf, x)
    return out


def _peer_score_kernel(xn_ref, wq_ref, sk1_ref, sk2_ref, s1_ref, s2_ref, a1_ref, a2_ref, thr_ref, cand_scr):
    half = PEER_DK // 2
    q_t = _dot_nt(wq_ref[...], xn_ref[...]).astype(BF16)
    for h in range(PEER_HEADS):
        s1 = _dot(sk1_ref[h], q_t[h * PEER_DK:h * PEER_DK + half])
        s2 = _dot(sk2_ref[h], q_t[h * PEER_DK + half:(h + 1) * PEER_DK])
        v1 = _extract_top(s1, PEER_TOPK)
        v2 = _extract_top(s2, PEER_TOPK)
        cand_scr[...] = jnp.full(cand_scr.shape, -jnp.inf, F32)
        for r, (i, j) in enumerate(PEER_CANDS):
            cand_scr[r:r + 1, :] = v1[i] + v2[j]
        top = _extract_top(cand_scr[...], PEER_TOPK)
        zsum = jnp.ones_like(top[0])
        for c in top[1:]:
            zsum = zsum + jnp.exp(c - top[0])
        s1_ref[h] = s1
        s2_ref[h * N_KEYS:(h + 1) * N_KEYS, :] = s2
        a1_ref[h] = jnp.exp(s1 - v1[0]) / zsum
        a2_ref[h * N_KEYS:(h + 1) * N_KEYS, :] = jnp.exp(s2 - v2[0])
        thr_ref[h:h + 1, :] = top[PEER_TOPK - 1]


def _peer_scores(xn, wq_t, sk1, sk2):
    t = xn.shape[0]
    tm = min(t, 256)
    fix2 = lambda i: (0, 0)
    fix3 = lambda i: (0, 0, 0)
    hk = PEER_HEADS * N_KEYS
    return pl.pallas_call(
        _peer_score_kernel,
        grid=(t // tm,),
        in_specs=[pl.BlockSpec((tm, D_MODEL), lambda i: (i, 0)), pl.BlockSpec(wq_t.shape, fix2),
                  pl.BlockSpec(sk1.shape, fix3), pl.BlockSpec(sk2.shape, fix3)],
        out_specs=[pl.BlockSpec((PEER_HEADS, N_KEYS, tm), lambda i: (0, 0, i)), pl.BlockSpec((hk, tm), lambda i: (0, i)),
                   pl.BlockSpec((PEER_HEADS, N_KEYS, tm), lambda i: (0, 0, i)), pl.BlockSpec((hk, tm), lambda i: (0, i)),
                   pl.BlockSpec((PEER_HEADS, tm), lambda i: (0, i))],
        out_shape=[jax.ShapeDtypeStruct((PEER_HEADS, N_KEYS, t), F32), jax.ShapeDtypeStruct((hk, t), F32),
                   jax.ShapeDtypeStruct((PEER_HEADS, N_KEYS, t), F32), jax.ShapeDtypeStruct((hk, t), F32),
                   jax.ShapeDtypeStruct((PEER_HEADS, t), F32)],
        scratch_shapes=[pltpu.VMEM((CAND_ROWS, tm), F32)],
        compiler_params=_cparams(("parallel",)),
        name="peer_scores",
    )(xn, wq_t, sk1, sk2)


PEER_TE = 1024
KEYS_PER_STEP = PEER_TE // N_KEYS


def _peer_dense_kernel(xn_ref, u_ref, vt_ref, s1_ref, s2_ref, a1_ref, a2_ref, thr_ref, o_ref, p_scr):
    j = pl.program_id(1)

    @pl.when(j == 0)
    def _():
        o_ref[...] = jnp.zeros(o_ref.shape, F32)

    h_t = _dot_nt(u_ref[...], xn_ref[...])
    act = 0.5 * h_t * (1.0 + lax.erf(h_t * (1.0 / math.sqrt(2.0))))
    for aa in range(KEYS_PER_STEP):
        w = jnp.zeros((N_KEYS, h_t.shape[1]), F32)
        for h in range(PEER_HEADS):
            lo = h * N_KEYS
            ssum = s2_ref[lo:lo + N_KEYS, :] + s1_ref[h, aa:aa + 1, :]
            gate = a2_ref[lo:lo + N_KEYS, :] * a1_ref[h, aa:aa + 1, :]
            w = w + jnp.where(ssum >= thr_ref[h:h + 1, :], gate, 0.0)
        p_scr[aa * N_KEYS:(aa + 1) * N_KEYS, :] = (w * act[aa * N_KEYS:(aa + 1) * N_KEYS]).astype(BF16)
    o_ref[...] += _dot(vt_ref[...], p_scr[...])


def _peer_dense(xn, u, v_t, s1, s2, a1, a2, thr):
    t = xn.shape[0]
    tm = min(t, 512)
    hk = PEER_HEADS * N_KEYS
    return pl.pallas_call(
        _peer_dense_kernel,
        grid=(t // tm, N_EXPERTS // PEER_TE),
        in_specs=[pl.BlockSpec((tm, D_MODEL), lambda i, j: (i, 0)),
                  pl.BlockSpec((PEER_TE, D_MODEL), lambda i, j: (j, 0)),
                  pl.BlockSpec((D_MODEL, PEER_TE), lambda i, j: (0, j)),
                  pl.BlockSpec((PEER_HEADS, KEYS_PER_STEP, tm), lambda i, j: (0, j, i)),
                  pl.BlockSpec((hk, tm), lambda i, j: (0, i)),
                  pl.BlockSpec((PEER_HEADS, KEYS_PER_STEP, tm), lambda i, j: (0, j, i)),
                  pl.BlockSpec((hk, tm), lambda i, j: (0, i)),
                  pl.BlockSpec((PEER_HEADS, tm), lambda i, j: (0, i))],
        out_specs=pl.BlockSpec((D_MODEL, tm), lambda i, j: (0, i)),
        out_shape=jax.ShapeDtypeStruct((D_MODEL, t), F32),
        scratch_shapes=[pltpu.VMEM((PEER_TE, tm), BF16)],
        compiler_params=_cparams(("parallel", "arbitrary")),
        name="peer_dense",
    )(xn, u, v_t, s1, s2, a1, a2, thr)


def _residual_t_kernel(x_ref, ot_ref, g_ref, xo_ref, xn_ref):
    x = x_ref[...] + ot_ref[...].T
    xo_ref[...] = x
    xn_ref[...] = ((x * lax.rsqrt(jnp.mean(x * x, axis=-1, keepdims=True) + EPS)) * g_ref[...]).astype(xn_ref.dtype)


def _residual_t(x, o_t, g, norm_dtype):
    t, d = x.shape
    tm = min(t, 256)
    return pl.pallas_call(
        _residual_t_kernel,
        grid=(t // tm,),
        in_specs=[pl.BlockSpec((tm, d), lambda i: (i, 0)), pl.BlockSpec((d, tm), lambda i: (0, i)),
                  pl.BlockSpec((1, d), lambda i: (0, 0))],
        out_specs=[pl.BlockSpec((tm, d), lambda i: (i, 0)), pl.BlockSpec((tm, d), lambda i: (i, 0))],
        out_shape=[jax.ShapeDtypeStruct((t, d), F32), jax.ShapeDtypeStruct((t, d), norm_dtype)],
        compiler_params=_cparams(("parallel",)),
        name="peer_residual",
    )(x, o_t, g.reshape(1, d))


def kernel(x_prompt, x_sample, cache_k, cache_v, state_ssm, state_conv, page_table, rel_bias, norm_attn_g, w_in, conv_w, conv_b, a_log, dt_bias, d_skip, ssm_norm_g, lam_qk, subln_g, w_out, norm_ffn_g, peer_wq, peer_sk1, peer_sk2, peer_u, peer_v, norm_final_g):
    bp, seq, d = x_prompt.shape
    bs, dec_seq, _ = x_sample.shape
    tp = bp * seq
    ts = bs * dec_seq
    ts_pad = LANES
    o1, o2, o3 = D_SSM, D_SSM + CONV_DIM, D_SSM + CONV_DIM + SSM_HEADS
    o4, o5 = o3 + D_ATTN, o3 + 2 * D_ATTN

    xp = x_prompt.reshape(tp, d)
    xs = jnp.pad(x_sample.reshape(ts, d), ((0, ts_pad - ts), (0, 0)))
    xnp = _norm(xp, norm_attn_g[0], BF16)
    xns = _norm(xs, norm_attn_g[0], BF16)

    conv0 = jnp.zeros((bp, CONV_W - 1, CONV_DIM), F32)
    ssm0 = jnp.zeros((bp, SSM_HEADS, SSM_HEAD_DIM, SSM_STATE), F32)
    outs_p = {k: [] for k in "kvhc"}
    outs_s = {k: [] for k in "kvhc"}
    y_prompt = y_sample = None

    for l in range(DEPTH):
        lam_init = 0.8 - 0.6 * math.exp(-0.3 * l)
        lq = lam_qk[l].astype(F32)
        lam = jnp.exp(jnp.sum(lq[0] * lq[1])) - jnp.exp(jnp.sum(lq[2] * lq[3])) + lam_init
        wl = w_in[l]
        w_ssm = jnp.concatenate([wl[:, :o2], jnp.pad(wl[:, o2:o3], ((0, 0), (0, LANES - SSM_HEADS)))], axis=1).astype(BF16)
        w_qkv = jnp.concatenate([wl[:, o3:o4] * (DA_HEAD_DIM ** -0.5), wl[:, o4:]], axis=1).astype(BF16)
        wo1 = w_out[l, :D_SSM].astype(BF16)
        wo2 = w_out[l, D_SSM:].astype(BF16)
        wq_t = peer_wq[l].T.astype(BF16)
        sk1 = peer_sk1[l].astype(BF16)
        sk2 = peer_sk2[l].astype(BF16)
        u_b = peer_u[l].astype(BF16)
        v_t = peer_v[l].T.astype(BF16)
        last = l == DEPTH - 1
        g_next = norm_final_g if last else norm_attn_g[l + 1]
        next_dtype = F32 if last else BF16

        z, xbc, dtr = _mm(xnp, w_ssm, (D_SSM, CONV_DIM, LANES), (F32, F32, F32), "inproj_ssm")
        q, k, v = _mm(xnp, w_qkv, (D_ATTN, D_ATTN, D_ATTN), (BF16, F32, F32), "inproj_qkv")
        y, hfin = _ssd(xbc.reshape(bp, seq, CONV_DIM), z.reshape(bp, seq, D_SSM), dtr.reshape(bp, seq, LANES),
                       conv_w[l], conv_b[l], dt_bias[l], a_log[l], d_skip[l], ssm_norm_g[l], ssm0, conv0, SSD_CHUNK)
        o = _attn_prompt(q, k, v, rel_bias, lam, subln_g[l], 1.0 - lam_init, bp, seq)
        xp, xn2 = _outproj(xp, y.reshape(tp, D_SSM), o, wo1, wo2, norm_ffn_g[l])
        s1, s2, a1, a2, thr = _peer_scores(xn2, wq_t, sk1, sk2)
        o_t = _peer_dense(xn2, u_b, v_t, s1, s2, a1, a2, thr)
        xp, xnp = _residual_t(xp, o_t, g_next, next_dtype)
        outs_p["k"].append(k.reshape(bp, seq, DA_HEADS, 2 * DA_HEAD_DIM))
        outs_p["v"].append(v.reshape(bp, seq, DA_HEADS, DA_V_DIM))
        outs_p["h"].append(hfin)
        outs_p["c"].append(xbc.reshape(bp, seq, CONV_DIM)[:, seq - (CONV_W - 1):])

        z, xbc, dtr = _mm(xns, w_ssm, (D_SSM, CONV_DIM, LANES), (F32, F32, F32), "inproj_ssm_s")
        q, k, v = _mm(xns, w_qkv, (D_ATTN, D_ATTN, D_ATTN), (BF16, F32, F32), "inproj_qkv_s")
        seq_pad = lambda a, n: jnp.pad(a[:ts].reshape(bs, dec_seq, a.shape[-1]), ((0, 0), (0, n - dec_seq), (0, 0)))
        y, hfin = _ssd(seq_pad(xbc, SSD_CHUNK), seq_pad(z, SSD_CHUNK), seq_pad(dtr, SSD_CHUNK),
                       conv_w[l], conv_b[l], dt_bias[l], a_log[l], d_skip[l], ssm_norm_g[l],
                       state_ssm[l], state_conv[l], dec_seq)
        q4 = q[:ts].reshape(bs, dec_seq, D_ATTN)
        o = _attn_sample(jnp.concatenate([q4, q4], axis=1), seq_pad(k, PAGE_SIZE), seq_pad(v, PAGE_SIZE),
                         cache_k, cache_v, page_table, l, rel_bias, lam, subln_g[l], 1.0 - lam_init, dec_seq)
        ys = jnp.pad(y[:, :dec_seq].reshape(ts, D_SSM), ((0, ts_pad - ts), (0, 0)))
        os_ = jnp.pad(o[:, :dec_seq].reshape(ts, D_ATTN), ((0, ts_pad - ts), (0, 0)))
        xs, xn2 = _outproj(xs, ys, os_, wo1, wo2, norm_ffn_g[l])
        s1, s2, a1, a2, thr = _peer_scores(xn2, wq_t, sk1, sk2)
        o_t = _peer_dense(xn2, u_b, v_t, s1, s2, a1, a2, thr)
        xs, xns = _residual_t(xs, o_t, g_next, next_dtype)
        xbc_s = xbc[:ts].reshape(bs, dec_seq, CONV_DIM)
        outs_s["k"].append(k[:ts].reshape(bs, dec_seq, DA_HEADS, 2 * DA_HEAD_DIM))
        outs_s["v"].append(v[:ts].reshape(bs, dec_seq, DA_HEADS, DA_V_DIM))
        outs_s["h"].append(hfin)
        outs_s["c"].append(jnp.concatenate([state_conv[l], xbc_s], axis=1)[:, -(CONV_W - 1):])
        if last:
            y_prompt = xnp.reshape(bp, seq, d)
            y_sample = xns[:ts].reshape(bs, dec_seq, d)

    return (y_prompt, y_sample,
            jnp.stack(outs_p["k"]), jnp.stack(outs_p["v"]), jnp.stack(outs_p["h"]), jnp.stack(outs_p["c"]),
            jnp.stack(outs_s["k"]), jnp.stack(outs_s["v"]), jnp.stack(outs_s["h"]), jnp.stack(outs_s["c"]))
```

```python
import functools
import math

import numpy as np
import jax
import jax.numpy as jnp
from jax import lax
from jax.experimental import pallas as pl
from jax.experimental.pallas import tpu as pltpu

D_MODEL = 2048
DEPTH = 2
PAGE_SIZE = 128
D_SSM = 1024
SSM_HEAD_DIM = 64
SSM_HEADS = 16
SSM_GROUPS = 2
SSM_STATE = 128
CONV_W = 4
CONV_DIM = D_SSM + 2 * SSM_GROUPS * SSM_STATE
SSD_CHUNK = 128
D_ATTN = 1024
DA_HEAD_DIM = 64
DA_V_DIM = 128
DA_HEADS = 8
REL_BUCKETS = 32
REL_MAX_DIST = 128
PEER_HEADS = 8
PEER_DK = 256
N_KEYS = 128
N_EXPERTS = N_KEYS * N_KEYS
PEER_TOPK = 16
EPS = 1e-6

LANES = 128
SUBLANES = 8
VMEM_LIMIT = 56 * 1024 * 1024
NEG = -1e30

F32 = jnp.float32
BF16 = jnp.bfloat16


def _cparams(sem):
    return pltpu.CompilerParams(dimension_semantics=sem, vmem_limit_bytes=VMEM_LIMIT)


def _dot(a, b):
    return jnp.dot(a, b, preferred_element_type=F32)


def _dot_nt(a, b):
    return lax.dot_general(a, b, (((1,), (1,)), ((), ())), preferred_element_type=F32)


def _rel_thresholds():
    n = np.arange(0, REL_MAX_DIST + 1)
    max_exact = REL_BUCKETS // 2
    nf = np.maximum(n, 1).astype(np.float32)
    large = max_exact + (np.log(nf / np.float32(max_exact)) / np.float32(math.log(REL_MAX_DIST / max_exact))
                         * np.float32(REL_BUCKETS - max_exact)).astype(np.int32)
    large = np.minimum(large, REL_BUCKETS - 1)
    bucket = np.where(n < max_exact, n, large)
    thr = []
    for b in range(1, REL_BUCKETS):
        thr.append(int(n[bucket >= b][0]))
    return thr


_REL_THR = _rel_thresholds()


def _bias_from_dist(dist, rb_ref, h):
    out = jnp.full(dist.shape, rb_ref[0, h], F32)
    for b in range(1, REL_BUCKETS):
        out = jnp.where(dist >= _REL_THR[b - 1], rb_ref[b, h], out)
    return out


def _norm_kernel(x_ref, g_ref, o_ref):
    x = x_ref[...]
    o_ref[...] = ((x * lax.rsqrt(jnp.mean(x * x, axis=-1, keepdims=True) + EPS)) * g_ref[...]).astype(o_ref.dtype)


def _norm(x, g, out_dtype):
    t, d = x.shape
    tm = min(t, 512)
    return pl.pallas_call(
        _norm_kernel,
        grid=(t // tm,),
        in_specs=[pl.BlockSpec((tm, d), lambda i: (i, 0)), pl.BlockSpec((1, d), lambda i: (0, 0))],
        out_specs=pl.BlockSpec((tm, d), lambda i: (i, 0)),
        out_shape=jax.ShapeDtypeStruct((t, d), out_dtype),
        compiler_params=_cparams(("parallel",)),
        name="rmsnorm",
    )(x, g.reshape(1, d))


def _mm_kernel(a_ref, w_ref, *o_refs, splits):
    a = a_ref[...]
    for (s, e), o_ref in zip(splits, o_refs):
        o_ref[...] = _dot(a, w_ref[:, s:e]).astype(o_ref.dtype)


def _mm(a, w, widths, dtypes, name):
    t, k = a.shape
    tm = min(t, 512)
    splits, s = [], 0
    for wd in widths:
        splits.append((s, s + wd))
        s += wd
    return pl.pallas_call(
        functools.partial(_mm_kernel, splits=tuple(splits)),
        grid=(t // tm,),
        in_specs=[pl.BlockSpec((tm, k), lambda i: (i, 0)), pl.BlockSpec(w.shape, lambda i: (0, 0))],
        out_specs=[pl.BlockSpec((tm, wd), lambda i: (i, 0)) for wd in widths],
        out_shape=[jax.ShapeDtypeStruct((t, wd), dt) for wd, dt in zip(widths, dtypes)],
        compiler_params=_cparams(("parallel",)),
        name=name,
    )(a, w)


def _outproj_kernel(x_ref, y_ref, o_ref, w1_ref, w2_ref, g_ref, xo_ref, xn_ref):
    x = x_ref[...] + _dot(y_ref[...], w1_ref[...]) + _dot(o_ref[...], w2_ref[...])
    xo_ref[...] = x
    xn_ref[...] = ((x * lax.rsqrt(jnp.mean(x * x, axis=-1, keepdims=True) + EPS)) * g_ref[...]).astype(xn_ref.dtype)


def _outproj(x, y, o, w1, w2, g):
    t, d = x.shape
    tm = min(t, 512)
    row = lambda i: (i, 0)
    fix = lambda i: (0, 0)
    return pl.pallas_call(
        _outproj_kernel,
        grid=(t // tm,),
        in_specs=[pl.BlockSpec((tm, d), row), pl.BlockSpec((tm, D_SSM), row), pl.BlockSpec((tm, D_ATTN), row),
                  pl.BlockSpec(w1.shape, fix), pl.BlockSpec(w2.shape, fix), pl.BlockSpec((1, d), fix)],
        out_specs=[pl.BlockSpec((tm, d), row), pl.BlockSpec((tm, d), row)],
        out_shape=[jax.ShapeDtypeStruct((t, d), F32), jax.ShapeDtypeStruct((t, d), BF16)],
        compiler_params=_cparams(("parallel",)),
        name="outproj",
    )(x, y, o, w1, w2, g.reshape(1, d))


def _ssd_kernel(xbc_ref, z_ref, dt_ref, cw_ref, cb_ref, dtb_ref, a_ref, dsk_ref, ng_ref, h0_ref, cbuf_ref,
                y_ref, hfin_ref, xpad_scr, h_scr, *, valid_len):
    c = pl.program_id(1)
    lc = SSD_CHUNK
    pairs = SSM_HEADS // 2

    @pl.when(c == 0)
    def _():
        xpad_scr[0:SUBLANES, :] = cbuf_ref[0]
        for j in range(pairs):
            hp = jnp.concatenate([h0_ref[0, 2 * j], h0_ref[0, 2 * j + 1]], axis=0)
            h_scr[j] = hp.T

    xpad_scr[SUBLANES:SUBLANES + lc, :] = xbc_ref[0]
    acc = jnp.broadcast_to(cb_ref[...], (lc, CONV_DIM))
    for w in range(CONV_W):
        lo = SUBLANES - (CONV_W - 1) + w
        acc = acc + xpad_scr[lo:lo + lc, :] * cw_ref[w:w + 1, :]
    xpad_scr[0:SUBLANES, :] = xpad_scr[lc:lc + SUBLANES, :]
    xc = acc * jax.nn.sigmoid(acc)

    gn = SSM_GROUPS * SSM_STATE
    xs = xc[:, :D_SSM]
    bm = xc[:, D_SSM:D_SSM + gn]
    cm = xc[:, D_SSM + gn:]

    rows = lax.broadcasted_iota(jnp.int32, (lc, LANES), 0)
    cols = lax.broadcasted_iota(jnp.int32, (lc, LANES), 1)
    tril = rows >= cols
    lane_lo = cols < SSM_HEAD_DIM

    xdt_in = dt_ref[0] + dtb_ref[...]
    dt = jnp.maximum(xdt_in, 0.0) + jnp.log1p(jnp.exp(-jnp.abs(xdt_in)))
    dt = jnp.where(rows < valid_len, dt, 0.0)
    da = dt * a_ref[...]
    cum = jnp.dot(tril.astype(F32), da, preferred_element_type=F32, precision=lax.Precision.HIGHEST)
    cum_t = cum.T
    last = cum[lc - 1:lc, :]

    cb, bt = [], []
    for g in range(SSM_GROUPS):
        b_g = bm[:, g * SSM_STATE:(g + 1) * SSM_STATE]
        c_g = cm[:, g * SSM_STATE:(g + 1) * SSM_STATE]
        cb.append(_dot_nt(c_g.astype(BF16), b_g.astype(BF16)))
        bt.append(b_g.T.astype(BF16))

    z = z_ref[0]
    gate = z * jax.nn.sigmoid(z)
    ys = []
    for j in range(pairs):
        ha, hb = 2 * j, 2 * j + 1
        g = ha // (SSM_HEADS // SSM_GROUPS)
        c_g = cm[:, g * SSM_STATE:(g + 1) * SSM_STATE]
        cum_a, cum_b = cum[:, ha:ha + 1], cum[:, hb:hb + 1]
        lm_a = jnp.exp(jnp.where(tril, cum_a - cum_t[ha:ha + 1, :], NEG))
        lm_b = jnp.exp(jnp.where(tril, cum_b - cum_t[hb:hb + 1, :], NEG))
        sc = jnp.concatenate([cb[g] * lm_a, cb[g] * lm_b], axis=1).astype(BF16)
        xs_p = xs[:, j * LANES:(j + 1) * LANES]
        xdt = xs_p * jnp.where(lane_lo, dt[:, ha:ha + 1], dt[:, hb:hb + 1])
        rhs = jnp.concatenate([jnp.where(lane_lo, xdt, 0.0), jnp.where(lane_lo, 0.0, xdt)], axis=0).astype(BF16)
        y_diag = _dot(sc, rhs)
        last_a, last_b = last[:, ha:ha + 1], last[:, hb:hb + 1]
        dte = jnp.exp(jnp.where(lane_lo, last_a - cum_a, last_b - cum_b))
        st = _dot(bt[g], (xdt * dte).astype(BF16))
        cexp = jnp.concatenate([c_g * jnp.exp(cum_a), c_g * jnp.exp(cum_b)], axis=1).astype(BF16)
        h_t = h_scr[j]
        rhs2 = jnp.concatenate([jnp.where(lane_lo, h_t, 0.0), jnp.where(lane_lo, 0.0, h_t)], axis=0).astype(BF16)
        y_off = _dot(cexp, rhs2)
        decay = jnp.where(lane_lo[0:1, :], jnp.exp(last_a), jnp.exp(last_b))
        h_scr[j] = decay * h_t + st
        dsk = jnp.where(lane_lo[0:1, :], dsk_ref[0:1, ha:ha + 1], dsk_ref[0:1, hb:hb + 1])
        ys.append(y_diag + y_off + dsk * xs_p)

    half = D_SSM // SSM_GROUPS
    outs = []
    for g in range(SSM_GROUPS):
        yg = jnp.concatenate(ys[g * (pairs // SSM_GROUPS):(g + 1) * (pairs // SSM_GROUPS)], axis=1)
        yg = yg * gate[:, g * half:(g + 1) * half]
        yg = yg * lax.rsqrt(jnp.mean(yg * yg, axis=-1, keepdims=True) + EPS)
        outs.append(yg * ng_ref[:, g * half:(g + 1) * half])
    y_ref[0] = jnp.concatenate(outs, axis=1).astype(y_ref.dtype)

    @pl.when(c == pl.num_programs(1) - 1)
    def _():
        for j in range(pairs):
            t = h_scr[j].T
            hfin_ref[0, 2 * j] = t[:SSM_HEAD_DIM]
            hfin_ref[0, 2 * j + 1] = t[SSM_HEAD_DIM:]


def _ssd(xbc, z, dt, conv_w, conv_b, dt_bias, a_log, d_skip, norm_g, h0, cbuf, valid_len):
    b, seq, _ = xbc.shape
    nc = seq // SSD_CHUNK
    pad1 = lambda v: jnp.pad(v.astype(F32), (0, LANES - SSM_HEADS)).reshape(1, LANES)
    cbuf8 = jnp.pad(cbuf, ((0, 0), (SUBLANES - (CONV_W - 1), 0), (0, 0)))
    blk = lambda w: pl.BlockSpec((1, SSD_CHUNK, w), lambda i, c: (i, c, 0))
    fix = lambda shape: pl.BlockSpec(shape, lambda i, c: tuple(0 for _ in shape))
    per_b = lambda shape: pl.BlockSpec((1,) + shape, lambda i, c: (i,) + tuple(0 for _ in shape))
    return pl.pallas_call(
        functools.partial(_ssd_kernel, valid_len=valid_len),
        grid=(b, nc),
        in_specs=[blk(CONV_DIM), blk(D_SSM), blk(LANES),
                  fix((CONV_W, CONV_DIM)), fix((1, CONV_DIM)), fix((1, LANES)), fix((1, LANES)), fix((1, LANES)),
                  fix((1, D_SSM)), per_b((SSM_HEADS, SSM_HEAD_DIM, SSM_STATE)), per_b((SUBLANES, CONV_DIM))],
        out_specs=[blk(D_SSM), per_b((SSM_HEADS, SSM_HEAD_DIM, SSM_STATE))],
        out_shape=[jax.ShapeDtypeStruct((b, seq, D_SSM), BF16),
                   jax.ShapeDtypeStruct((b, SSM_HEADS, SSM_HEAD_DIM, SSM_STATE), F32)],
        scratch_shapes=[pltpu.VMEM((SUBLANES + SSD_CHUNK, CONV_DIM), F32),
                        pltpu.VMEM((SSM_HEADS // 2, SSM_STATE, 2 * SSM_HEAD_DIM), F32)],
        compiler_params=_cparams(("parallel", "arbitrary")),
        name="conv_ssd",
    )(xbc, z, dt, conv_w, conv_b.reshape(1, CONV_DIM), pad1(dt_bias), pad1(-jnp.exp(a_log.astype(F32))),
      pad1(d_skip), norm_g.reshape(1, D_SSM), h0, cbuf8)


ATT_T = 512


def _softmax_step(s, v, m_scr, l_scr, acc_scr):
    reps = s.shape[1] // LANES
    m_prev = m_scr[...]
    m_new = jnp.maximum(m_prev, jnp.max(s, axis=1, keepdims=True))
    alpha = jnp.exp(m_prev - m_new)
    p = jnp.exp(s - jnp.concatenate([m_new] * reps, axis=1))
    l_scr[...] = alpha * l_scr[...] + jnp.sum(p, axis=1, keepdims=True)
    acc_scr[...] = alpha * acc_scr[...] + _dot(p.astype(BF16), v)
    m_scr[...] = m_new


def _attn_p_kernel(rb_ref, lam_ref, q_ref, k_ref, v_ref, g_ref, o_ref,
                   r_scr, m1, l1, a1, m2, l2, a2, *, out_scale):
    h, b, qi, ki = pl.program_id(0), pl.program_id(1), pl.program_id(2), pl.program_id(3)
    t = ATT_T
    rows = lax.broadcasted_iota(jnp.int32, (t, t), 0)
    cols = lax.broadcasted_iota(jnp.int32, (t, t), 1)

    @pl.when((b == 0) & (qi == 0) & (ki == 0))
    def _():
        r_scr[...] = _bias_from_dist((rows - cols) & (t - 1), rb_ref, h)

    @pl.when(ki == 0)
    def _():
        for m, l, a in ((m1, l1, a1), (m2, l2, a2)):
            m[...] = jnp.full(m.shape, NEG, F32)
            l[...] = jnp.zeros(l.shape, F32)
            a[...] = jnp.zeros(a.shape, F32)

    def update(bias):
        q = q_ref[...]
        k = k_ref[...].astype(BF16)
        v = v_ref[...].astype(BF16)
        lane_lo = lax.broadcasted_iota(jnp.int32, q.shape, 1) < DA_HEAD_DIM
        zero = jnp.zeros_like(q)
        s1 = _dot_nt(jnp.where(lane_lo, q, zero), k) + bias
        _softmax_step(s1, v, m1, l1, a1)
        s2 = _dot_nt(jnp.where(lane_lo, zero, q), k) + bias
        _softmax_step(s2, v, m2, l2, a2)

    far = rb_ref[REL_BUCKETS - 1, h]

    @pl.when(ki == qi)
    def _():
        update(jnp.where(cols <= rows, r_scr[...], NEG))

    @pl.when(ki == qi - 1)
    def _():
        update(jnp.where(cols > rows, r_scr[...], far))

    @pl.when(ki < qi - 1)
    def _():
        update(far)

    @pl.when(ki == qi)
    def _():
        o = a1[...] / l1[...] - lam_ref[0] * (a2[...] / l2[...])
        o = (o * lax.rsqrt(jnp.mean(o * o, axis=-1, keepdims=True) + EPS)) * g_ref[...]
        o_ref[...] = (o * out_scale).astype(o_ref.dtype)


def _attn_prompt(q, k, v, rel_bias, lam, subln_g, out_scale, batch, seq):
    t = ATT_T
    nq = seq // t
    smem = pl.BlockSpec(memory_space=pltpu.SMEM)
    qmap = lambda h, b, qi, ki: (b * nq + qi, h)
    kmap = lambda h, b, qi, ki: (b * nq + jnp.minimum(ki, qi), h)
    stat = pltpu.VMEM((t, LANES), F32)
    return pl.pallas_call(
        functools.partial(_attn_p_kernel, out_scale=out_scale),
        grid=(DA_HEADS, batch, nq, nq),
        in_specs=[smem, smem, pl.BlockSpec((t, LANES), qmap), pl.BlockSpec((t, LANES), kmap),
                  pl.BlockSpec((t, LANES), kmap), pl.BlockSpec((1, LANES), lambda h, b, qi, ki: (0, 0))],
        out_specs=pl.BlockSpec((t, LANES), qmap),
        out_shape=jax.ShapeDtypeStruct((batch * seq, D_ATTN), BF16),
        scratch_shapes=[pltpu.VMEM((t, t), F32), stat, stat, stat, stat, stat, stat],
        compiler_params=_cparams(("arbitrary", "arbitrary", "arbitrary", "arbitrary")),
        name="diff_attn_prompt",
    )(rel_bias, lam.reshape(1), q, k, v, subln_g.reshape(1, LANES))


PAGES_PER_STEP = 8
DEC_ROWS = SUBLANES
DEC_QROWS = DA_HEADS * DEC_ROWS


def _attn_s_kernel(pt_ref, lam_ref, q_ref, kn_ref, vn_ref, rbx_ref, g_ref, *refs, dec_seq, out_scale):
    del pt_ref
    pps = PAGES_PER_STEP
    k_refs, v_refs = refs[:pps], refs[pps:2 * pps]
    o_ref, m_scr, l_scr, acc_scr = refs[2 * pps:]
    s = pl.program_id(1)
    nsteps = pl.num_programs(1)

    @pl.when(s == 0)
    def _():
        m_scr[...] = jnp.full(m_scr.shape, NEG, F32)
        l_scr[...] = jnp.zeros(l_scr.shape, F32)
        acc_scr[...] = jnp.zeros(acc_scr.shape, F32)

    q = q_ref[0]

    def geometry(n):
        rows = lax.broadcasted_iota(jnp.int32, (DEC_QROWS, n), 0)
        lanes = lax.broadcasted_iota(jnp.int32, (DEC_QROWS, n), 1)
        r8 = rows & (DEC_ROWS - 1)
        qidx = jnp.where(r8 < dec_seq, r8, r8 - dec_seq)
        own = (lanes & (DA_HEADS - 1)) == (rows >> 3)
        return qidx, lanes >> 3, own

    def shifted_bias(dist):
        reps = dist.shape[1] // LANES
        tile = lambda b: jnp.concatenate([rbx_ref[b]] * reps, axis=1)
        out = tile(0)
        for b in range(1, REL_BUCKETS):
            out = jnp.where(dist >= _REL_THR[b - 1], tile(b), out)
        return out

    def update(k2d, v2d, keep, bias):
        sc = _dot_nt(q, k2d)
        if bias is not None:
            sc = sc + bias
        sc = jnp.where(keep, sc, NEG)
        reps = sc.shape[1] // LANES
        m_prev = m_scr[...]
        m_new = jnp.maximum(m_prev, jnp.max(sc, axis=1, keepdims=True))
        alpha = jnp.exp(m_prev - m_new)
        p = jnp.exp(sc - jnp.concatenate([m_new] * reps, axis=1))
        l_scr[...] = alpha * l_scr[...] + jnp.sum(p, axis=1, keepdims=True)
        acc_scr[...] = alpha * acc_scr[...] + _dot(p.astype(BF16), v2d)
        m_scr[...] = m_new

    def page(r, with_bias):
        n = PAGE_SIZE * DA_HEADS
        k2d = k_refs[r][...].reshape(n, LANES).astype(BF16)
        v2d = v_refs[r][...].reshape(n, LANES).astype(BF16)
        qidx, pos, own = geometry(n)
        bias = shifted_bias(PAGE_SIZE + qidx - pos) if with_bias else None
        update(k2d, v2d, own, bias)

    for r in range(pps - 1):
        page(r, False)

    @pl.when(s < nsteps - 1)
    def _():
        page(pps - 1, False)

    @pl.when(s == nsteps - 1)
    def _():
        page(pps - 1, True)
        qidx, pos, own = geometry(LANES)
        keep = own & (pos < dec_seq) & (pos <= qidx)
        update(kn_ref[0].astype(BF16), vn_ref[0].astype(BF16), keep, shifted_bias(jnp.maximum(qidx - pos, 0)))
        t = acc_scr[...] / l_scr[...]
        o = t - lam_ref[0] * pltpu.roll(t, DEC_QROWS - dec_seq, axis=0)
        o = (o * lax.rsqrt(jnp.mean(o * o, axis=-1, keepdims=True) + EPS)) * g_ref[...]
        o_ref[0] = (o * out_scale).astype(o_ref.dtype)


def _attn_sample(q, k_new, v_new, cache_k, cache_v, page_table, layer, rel_bias, lam, subln_g, out_scale, dec_seq):
    batch, n_pages = page_table.shape
    pps = PAGES_PER_STEP
    qh = q.reshape(batch, dec_seq, DA_HEADS, LANES).transpose(0, 2, 1, 3)
    lo = lax.broadcasted_iota(jnp.int32, qh.shape, 3) < DA_HEAD_DIM
    zero = jnp.zeros_like(qh)
    q_all = jnp.concatenate([jnp.where(lo, qh, zero), jnp.where(lo, zero, qh)], axis=2).reshape(batch, DEC_QROWS, LANES)
    new_rows = lambda a: jnp.pad(a.reshape(batch, dec_seq * DA_HEADS, LANES),
                                 ((0, 0), (0, LANES - dec_seq * DA_HEADS), (0, 0)))
    rbx = rel_bias - rel_bias[REL_BUCKETS - 1:REL_BUCKETS]
    rbx = jnp.broadcast_to(jnp.repeat(rbx, DEC_ROWS, axis=1)[:, :, None], (REL_BUCKETS, DEC_QROWS, LANES))
    smem = pl.BlockSpec(memory_space=pltpu.SMEM)
    row = lambda b, s, pt: (b, 0, 0)
    page_specs = [pl.BlockSpec((None, None, PAGE_SIZE, DA_HEADS, LANES),
                               functools.partial(lambda b, s, pt, r: (layer, pt[b, s * pps + r], 0, 0, 0), r=r))
                  for r in range(pps)]
    stat = pltpu.VMEM((DEC_QROWS, LANES), F32)
    grid_spec = pltpu.PrefetchScalarGridSpec(
        num_scalar_prefetch=1,
        grid=(batch, n_pages // pps),
        in_specs=[smem, pl.BlockSpec((1, DEC_QROWS, LANES), row), pl.BlockSpec((1, LANES, LANES), row),
                  pl.BlockSpec((1, LANES, LANES), row),
                  pl.BlockSpec((REL_BUCKETS, DEC_QROWS, LANES), lambda b, s, pt: (0, 0, 0)),
                  pl.BlockSpec((1, LANES), lambda b, s, pt: (0, 0))]
                 + page_specs + page_specs,
        out_specs=pl.BlockSpec((1, DEC_QROWS, LANES), row),
        scratch_shapes=[stat, stat, stat],
    )
    o = pl.pallas_call(
        functools.partial(_attn_s_kernel, dec_seq=dec_seq, out_scale=out_scale),
        grid_spec=grid_spec,
        out_shape=jax.ShapeDtypeStruct((batch, DEC_QROWS, LANES), BF16),
        compiler_params=_cparams(("parallel", "arbitrary")),
        name="diff_attn_sample",
    )(page_table, lam.reshape(1), q_all, new_rows(k_new), new_rows(v_new), rbx, subln_g.reshape(1, LANES),
      *([cache_k] * pps), *([cache_v] * pps))
    o = o.reshape(batch, DA_HEADS, DEC_ROWS, LANES)[:, :, :dec_seq].transpose(0, 2, 1, 3)
    return o.reshape(batch * dec_seq, D_ATTN)


PEER_CANDS = [(i, j) for i in range(PEER_TOPK) for j in range(PEER_TOPK) if (i + 1) * (j + 1) <= PEER_TOPK]
CAND_ROWS = -(-len(PEER_CANDS) // SUBLANES) * SUBLANES


def _extract_top(x, count):
    rows = lax.broadcasted_iota(jnp.int32, x.shape, 0)
    out = []
    for r in range(count):
        m = jnp.max(x, axis=0, keepdims=True)
        out.append(m)
        if r + 1 < count:
            first = jnp.min(jnp.where(x == m, rows, x.shape[0]), axis=0, keepdims=True)
            x = jnp.where(rows == first, -jnp.inf, x)
    return out


def _peer_score_kernel(xn_ref, wq_ref, sk1_ref, sk2_ref, s1_ref, s2_ref, a1_ref, a2_ref, thr_ref, cand_scr):
    half = PEER_DK // 2
    q_t = _dot_nt(wq_ref[...], xn_ref[...]).astype(BF16)
    for h in range(PEER_HEADS):
        s1 = _dot(sk1_ref[h], q_t[h * PEER_DK:h * PEER_DK + half])
        s2 = _dot(sk2_ref[h], q_t[h * PEER_DK + half:(h + 1) * PEER_DK])
        v1 = _extract_top(s1, PEER_TOPK)
        v2 = _extract_top(s2, PEER_TOPK)
        cand_scr[...] = jnp.full(cand_scr.shape, -jnp.inf, F32)
        for r, (i, j) in enumerate(PEER_CANDS):
            cand_scr[r:r + 1, :] = v1[i] + v2[j]
        top = _extract_top(cand_scr[...], PEER_TOPK)
        zsum = jnp.ones_like(top[0])
        for c in top[1:]:
            zsum = zsum + jnp.exp(c - top[0])
        s1_ref[h] = s1
        s2_ref[h * N_KEYS:(h + 1) * N_KEYS, :] = s2
        a1_ref[h] = jnp.exp(s1 - v1[0]) / zsum
        a2_ref[h * N_KEYS:(h + 1) * N_KEYS, :] = jnp.exp(s2 - v2[0])
        thr_ref[h:h + 1, :] = top[PEER_TOPK - 1]


def _peer_scores(xn, wq_t, sk1, sk2):
    t = xn.shape[0]
    tm = min(t, 256)
    fix2 = lambda i: (0, 0)
    fix3 = lambda i: (0, 0, 0)
    hk = PEER_HEADS * N_KEYS
    return pl.pallas_call(
        _peer_score_kernel,
        grid=(t // tm,),
        in_specs=[pl.BlockSpec((tm, D_MODEL), lambda i: (i, 0)), pl.BlockSpec(wq_t.shape, fix2),
                  pl.BlockSpec(sk1.shape, fix3), pl.BlockSpec(sk2.shape, fix3)],
        out_specs=[pl.BlockSpec((PEER_HEADS, N_KEYS, tm), lambda i: (0, 0, i)), pl.BlockSpec((hk, tm), lambda i: (0, i)),
                   pl.BlockSpec((PEER_HEADS, N_KEYS, tm), lambda i: (0, 0, i)), pl.BlockSpec((hk, tm), lambda i: (0, i)),
                   pl.BlockSpec((PEER_HEADS, tm), lambda i: (0, i))],
        out_shape=[jax.ShapeDtypeStruct((PEER_HEADS, N_KEYS, t), F32), jax.ShapeDtypeStruct((hk, t), F32),
                   jax.ShapeDtypeStruct((PEER_HEADS, N_KEYS, t), F32), jax.ShapeDtypeStruct((hk, t), F32),
                   jax.ShapeDtypeStruct((PEER_HEADS, t), F32)],
        scratch_shapes=[pltpu.VMEM((CAND_ROWS, tm), F32)],
        compiler_params=_cparams(("parallel",)),
        name="peer_scores",
    )(xn, wq_t, sk1, sk2)


PEER_TE = 1024
KEYS_PER_STEP = PEER_TE // N_KEYS


def _peer_dense_kernel(xn_ref, u_ref, vt_ref, s1_ref, s2_ref, a1_ref, a2_ref, thr_ref, o_ref, p_scr):
    j = pl.program_id(1)

    @pl.when(j == 0)
    def _():
        o_ref[...] = jnp.zeros(o_ref.shape, F32)

    h_t = _dot_nt(u_ref[...], xn_ref[...])
    act = 0.5 * h_t * (1.0 + lax.erf(h_t * (1.0 / math.sqrt(2.0))))
    for aa in range(KEYS_PER_STEP):
        w = jnp.zeros((N_KEYS, h_t.shape[1]), F32)
        for h in range(PEER_HEADS):
            lo = h * N_KEYS
            ssum = s2_ref[lo:lo + N_KEYS, :] + s1_ref[h, aa:aa + 1, :]
            gate = a2_ref[lo:lo + N_KEYS, :] * a1_ref[h, aa:aa + 1, :]
            w = w + jnp.where(ssum >= thr_ref[h:h + 1, :], gate, 0.0)
        p_scr[aa * N_KEYS:(aa + 1) * N_KEYS, :] = (w * act[aa * N_KEYS:(aa + 1) * N_KEYS]).astype(BF16)
    o_ref[...] += _dot(vt_ref[...], p_scr[...])


def _peer_dense(xn, u, v_t, s1, s2, a1, a2, thr):
    t = xn.shape[0]
    tm = min(t, 512)
    hk = PEER_HEADS * N_KEYS
    return pl.pallas_call(
        _peer_dense_kernel,
        grid=(t // tm, N_EXPERTS // PEER_TE),
        in_specs=[pl.BlockSpec((tm, D_MODEL), lambda i, j: (i, 0)),
                  pl.BlockSpec((PEER_TE, D_MODEL), lambda i, j: (j, 0)),
                  pl.BlockSpec((D_MODEL, PEER_TE), lambda i, j: (0, j)),
                  pl.BlockSpec((PEER_HEADS, KEYS_PER_STEP, tm), lambda i, j: (0, j, i)),
                  pl.BlockSpec((hk, tm), lambda i, j: (0, i)),
                  pl.BlockSpec((PEER_HEADS, KEYS_PER_STEP, tm), lambda i, j: (0, j, i)),
                  pl.BlockSpec((hk, tm), lambda i, j: (0, i)),
                  pl.BlockSpec((PEER_HEADS, tm), lambda i, j: (0, i))],
        out_specs=pl.BlockSpec((D_MODEL, tm), lambda i, j: (0, i)),
        out_shape=jax.ShapeDtypeStruct((D_MODEL, t), F32),
        scratch_shapes=[pltpu.VMEM((PEER_TE, tm), BF16)],
        compiler_params=_cparams(("parallel", "arbitrary")),
        name="peer_dense",
    )(xn, u, v_t, s1, s2, a1, a2, thr)


def _residual_t_kernel(x_ref, ot_ref, g_ref, xo_ref, xn_ref):
    x = x_ref[...] + ot_ref[...].T
    xo_ref[...] = x
    xn_ref[...] = ((x * lax.rsqrt(jnp.mean(x * x, axis=-1, keepdims=True) + EPS)) * g_ref[...]).astype(xn_ref.dtype)


def _residual_t(x, o_t, g, norm_dtype):
    t, d = x.shape
    tm = min(t, 256)
    return pl.pallas_call(
        _residual_t_kernel,
        grid=(t // tm,),
        in_specs=[pl.BlockSpec((tm, d), lambda i: (i, 0)), pl.BlockSpec((d, tm), lambda i: (0, i)),
                  pl.BlockSpec((1, d), lambda i: (0, 0))],
        out_specs=[pl.BlockSpec((tm, d), lambda i: (i, 0)), pl.BlockSpec((tm, d), lambda i: (i, 0))],
        out_shape=[jax.ShapeDtypeStruct((t, d), F32), jax.ShapeDtypeStruct((t, d), norm_dtype)],
        compiler_params=_cparams(("parallel",)),
        name="peer_residual",
    )(x, o_t, g.reshape(1, d))


def kernel(x_prompt, x_sample, cache_k, cache_v, state_ssm, state_conv, page_table, rel_bias, norm_attn_g, w_in, conv_w, conv_b, a_log, dt_bias, d_skip, ssm_norm_g, lam_qk, subln_g, w_out, norm_ffn_g, peer_wq, peer_sk1, peer_sk2, peer_u, peer_v, norm_final_g):
    bp, seq, d = x_prompt.shape
    bs, dec_seq, _ = x_sample.shape
    tp = bp * seq
    ts = bs * dec_seq
    ts_pad = LANES
    o1, o2, o3 = D_SSM, D_SSM + CONV_DIM, D_SSM + CONV_DIM + SSM_HEADS
    o4, o5 = o3 + D_ATTN, o3 + 2 * D_ATTN

    xp = x_prompt.reshape(tp, d)
    xs = jnp.pad(x_sample.reshape(ts, d), ((0, ts_pad - ts), (0, 0)))
    xnp = _norm(xp, norm_attn_g[0], BF16)
    xns = _norm(xs, norm_attn_g[0], BF16)

    conv0 = jnp.zeros((bp, CONV_W - 1, CONV_DIM), F32)
    ssm0 = jnp.zeros((bp, SSM_HEADS, SSM_HEAD_DIM, SSM_STATE), F32)
    outs_p = {k: [] for k in "kvhc"}
    outs_s = {k: [] for k in "kvhc"}
    y_prompt = y_sample = None

    for l in range(DEPTH):
        lam_init = 0.8 - 0.6 * math.exp(-0.3 * l)
        lq = lam_qk[l].astype(F32)
        lam = jnp.exp(jnp.sum(lq[0] * lq[1])) - jnp.exp(jnp.sum(lq[2] * lq[3])) + lam_init
        wl = w_in[l]
        w_ssm = jnp.concatenate([wl[:, :o2], jnp.pad(wl[:, o2:o3], ((0, 0), (0, LANES - SSM_HEADS)))], axis=1).astype(BF16)
        w_qkv = jnp.concatenate([wl[:, o3:o4] * (DA_HEAD_DIM ** -0.5), wl[:, o4:]], axis=1).astype(BF16)
        wo1 = w_out[l, :D_SSM].astype(BF16)
        wo2 = w_out[l, D_SSM:].astype(BF16)
        wq_t = peer_wq[l].T.astype(BF16)
        sk1 = peer_sk1[l].astype(BF16)
        sk2 = peer_sk2[l].astype(BF16)
        u_b = peer_u[l].astype(BF16)
        v_t = peer_v[l].T.astype(BF16)
        last = l == DEPTH - 1
        g_next = norm_final_g if last else norm_attn_g[l + 1]
        next_dtype = F32 if last else BF16

        z, xbc, dtr = _mm(xnp, w_ssm, (D_SSM, CONV_DIM, LANES), (F32, F32, F32), "inproj_ssm")
        q, k, v = _mm(xnp, w_qkv, (D_ATTN, D_ATTN, D_ATTN), (BF16, F32, F32), "inproj_qkv")
        y, hfin = _ssd(xbc.reshape(bp, seq, CONV_DIM), z.reshape(bp, seq, D_SSM), dtr.reshape(bp, seq, LANES),
                       conv_w[l], conv_b[l], dt_bias[l], a_log[l], d_skip[l], ssm_norm_g[l], ssm0, conv0, SSD_CHUNK)
        o = _attn_prompt(q, k, v, rel_bias, lam, subln_g[l], 1.0 - lam_init, bp, seq)
        xp, xn2 = _outproj(xp, y.reshape(tp, D_SSM), o, wo1, wo2, norm_ffn_g[l])
        s1, s2, a1, a2, thr = _peer_scores(xn2, wq_t, sk1, sk2)
        o_t = _peer_dense(xn2, u_b, v_t, s1, s2, a1, a2, thr)
        xp, xnp = _residual_t(xp, o_t, g_next, next_dtype)
        outs_p["k"].append(k.reshape(bp, seq, DA_HEADS, 2 * DA_HEAD_DIM))
        outs_p["v"].append(v.reshape(bp, seq, DA_HEADS, DA_V_DIM))
        outs_p["h"].append(hfin)
        outs_p["c"].append(xbc.reshape(bp, seq, CONV_DIM)[:, seq - (CONV_W - 1):])

        z, xbc, dtr = _mm(xns, w_ssm, (D_SSM, CONV_DIM, LANES), (F32, F32, F32), "inproj_ssm_s")
        q, k, v = _mm(xns, w_qkv, (D_ATTN, D_ATTN, D_ATTN), (BF16, F32, F32), "inproj_qkv_s")
        seq_pad = lambda a, n: jnp.pad(a[:ts].reshape(bs, dec_seq, a.shape[-1]), ((0, 0), (0, n - dec_seq), (0, 0)))
        y, hfin = _ssd(seq_pad(xbc, SSD_CHUNK), seq_pad(z, SSD_CHUNK), seq_pad(dtr, SSD_CHUNK),
                       conv_w[l], conv_b[l], dt_bias[l], a_log[l], d_skip[l], ssm_norm_g[l],
                       state_ssm[l], state_conv[l], dec_seq)
        o = _attn_sample(q[:ts], k[:ts], v[:ts], cache_k, cache_v, page_table, l, rel_bias, lam, subln_g[l],
                         1.0 - lam_init, dec_seq)
        ys = jnp.pad(y[:, :dec_seq].reshape(ts, D_SSM), ((0, ts_pad - ts), (0, 0)))
        os_ = jnp.pad(o, ((0, ts_pad - ts), (0, 0)))
        xs, xn2 = _outproj(xs, ys, os_, wo1, wo2, norm_ffn_g[l])
        s1, s2, a1, a2, thr = _peer_scores(xn2, wq_t, sk1, sk2)
        o_t = _peer_dense(xn2, u_b, v_t, s1, s2, a1, a2, thr)
        xs, xns = _residual_t(xs, o_t, g_next, next_dtype)
        xbc_s = xbc[:ts].reshape(bs, dec_seq, CONV_DIM)
        outs_s["k"].append(k[:ts].reshape(bs, dec_seq, DA_HEADS, 2 * DA_HEAD_DIM))
        outs_s["v"].append(v[:ts].reshape(bs, dec_seq, DA_HEADS, DA_V_DIM))
        outs_s["h"].append(hfin)
        outs_s["c"].append(jnp.concatenate([state_conv[l], xbc_s], axis=1)[:, -(CONV_W - 1):])
        if last:
            y_prompt = xnp.reshape(bp, seq, d)
            y_sample = xns[:ts].reshape(bs, dec_seq, d)

    return (y_prompt, y_sample,
            jnp.stack(outs_p["k"]), jnp.stack(outs_p["v"]), jnp.stack(outs_p["h"]), jnp.stack(outs_p["c"]),
            jnp.stack(outs_s["k"]), jnp.stack(outs_s["v"]), jnp.stack(outs_s["h"]), jnp.stack(outs_s["c"]))
```
